```python
import math
import jax, jax.numpy as jnp
from jax import lax
import numpy as np

D_MODEL = 2048
BATCH = 16
SEQ = 2048
DEPTH = 4

N_CONV_LAYERS = DEPTH // 2
N_ATTN_LAYERS = DEPTH - N_CONV_LAYERS
CONV_WIDTH = 31
D_FF = 4 * D_MODEL
HEAD_DIM = 128
N_HEADS = D_MODEL // HEAD_DIM
N_KV_HEADS = 4
GROUP = N_HEADS // N_KV_HEADS
ROPE_DIM = HEAD_DIM // 4
ROPE_THETA = 500000.0
CMP_BLOCK = 32
CMP_STRIDE = 16
CMP_HIDDEN = 4 * HEAD_DIM
SEL_BLOCK = 64
SEL_TOP_N = 16
WINDOW = 512
WIN_QBLOCK = 128
SEL_QCHUNK = 16
N_BRANCH = 3
N_KV_TENSORS = 6
FORCE_BONUS = 1000.0
EPS = 1e-6
NEG = -1e30

kernel_name = "yoco_conformer_nsa_hybrid"


def rmsnorm(x, g):
    xf = x.astype(jnp.float32)
    y = xf * lax.rsqrt(jnp.mean(xf * xf, axis=-1, keepdims=True) + EPS) * g.astype(jnp.float32)
    return y.astype(x.dtype)


def layernorm(x, g, b):
    xf = x.astype(jnp.float32)
    mu = jnp.mean(xf, axis=-1, keepdims=True)
    var = jnp.mean(jnp.square(xf - mu), axis=-1, keepdims=True)
    y = (xf - mu) * lax.rsqrt(var + EPS) * g.astype(jnp.float32) + b.astype(jnp.float32)
    return y.astype(x.dtype)


def rope_partial(x, pos):
    half = ROPE_DIM // 2
    inv = ROPE_THETA ** (-jnp.arange(0, ROPE_DIM, 2, dtype=jnp.float32) / ROPE_DIM)
    ang = pos.astype(jnp.float32)[:, None] * inv[None, :]
    cos, sin = jnp.cos(ang), jnp.sin(ang)
    xf = x.astype(jnp.float32)
    x1, x2 = xf[..., :half], xf[..., half:ROPE_DIM]
    out = jnp.concatenate([x1 * cos - x2 * sin, x2 * cos + x1 * sin, xf[..., ROPE_DIM:]], axis=-1)
    return out.astype(x.dtype)


def masked_softmax(s, mask):
    s = jnp.where(mask, s.astype(jnp.float32), NEG)
    p = jax.nn.softmax(s, axis=-1)
    return jnp.where(mask, p, 0.0)


def conv_module(h, w_pw1, b_pw1, w_dw, b_dw, ln_g, ln_b, w_pw2, b_pw2):
    u = h @ w_pw1 + b_pw1
    a, gate = jnp.split(u, 2, axis=-1)
    u = a * jax.nn.sigmoid(gate)
    u = lax.conv_general_dilated(u, w_dw[:, None, :], window_strides=(1,),
                                 padding=[(CONV_WIDTH - 1, 0)],
                                 dimension_numbers=("NWC", "WIO", "NWC"),
                                 feature_group_count=D_MODEL) + b_dw
    u = jax.nn.silu(layernorm(u, ln_g, ln_b))
    return u @ w_pw2 + b_pw2


def sq_relu_mlp(h, w_up, w_down):
    return jnp.square(jax.nn.relu(h @ w_up)) @ w_down


def compress_blocks(kb, w1, w2):
    B, G, NC = kb.shape[:3]
    flat = kb.reshape(B, G, NC, CMP_BLOCK * HEAD_DIM)
    return jax.nn.silu(flat @ w1) @ w2


def shared_kv(s, w_kv, cmp_pe_k, cmp_pe_v, cmp_k_w1, cmp_k_w2, cmp_v_w1, cmp_v_w2):
    B, S, _ = s.shape
    pos = jnp.arange(S)
    kv = (s @ w_kv).reshape(B, S, N_KV_TENSORS, N_KV_HEADS, HEAD_DIM).transpose(2, 0, 3, 1, 4)
    k_c, v_c, k_s, v_s, k_w, v_w = [kv[i] for i in range(N_KV_TENSORS)]
    n_cmp = (S - CMP_BLOCK) // CMP_STRIDE + 1
    idx = jnp.arange(n_cmp)[:, None] * CMP_STRIDE + jnp.arange(CMP_BLOCK)[None, :]
    k_cmp = compress_blocks(k_c[:, :, idx] + cmp_pe_k, cmp_k_w1, cmp_k_w2)
    v_cmp = compress_blocks(v_c[:, :, idx] + cmp_pe_v, cmp_v_w1, cmp_v_w2)
    return (k_cmp, v_cmp, rope_partial(k_s, pos), v_s, rope_partial(k_w, pos), v_w)


def nsa_mixer(h, kv, w_in, w_o):
    k_cmp, v_cmp, k_slc, v_slc, k_win, v_win = kv
    B, S, _ = h.shape
    G, R, dh = N_KV_HEADS, GROUP, HEAD_DIM
    pos = jnp.arange(S)
    scale = HEAD_DIM ** -0.5
    proj = h @ w_in
    q = proj[..., :N_HEADS * dh].reshape(B, S, G, R, dh).transpose(0, 2, 3, 1, 4)
    gates = jax.nn.sigmoid(proj[..., N_HEADS * dh:].astype(jnp.float32))
    gates = gates.reshape(B, S, G, R, N_BRANCH).transpose(0, 2, 3, 1, 4)
    q_rot = rope_partial(q, pos)

    n_cmp = k_cmp.shape[2]
    c_start = jnp.arange(n_cmp) * CMP_STRIDE
    mask_c = (c_start + CMP_BLOCK - 1)[None, :] <= pos[:, None]
    s_c = jnp.einsum("bgrtd,bgnd->bgrtn", q, k_cmp) * scale
    p_c = masked_softmax(s_c, mask_c)
    o_cmp = jnp.einsum("bgrtn,bgnd->bgrtd", p_c.astype(v_cmp.dtype), v_cmp)

    n_sel = S // SEL_BLOCK
    s_start = jnp.arange(n_sel) * SEL_BLOCK
    overlap = ((c_start[:, None] < s_start[None, :] + SEL_BLOCK) &
               (c_start[:, None] + CMP_BLOCK > s_start[None, :])).astype(jnp.float32)
    imp = jnp.einsum("bgrtn,nj->bgtj", p_c, overlap)
    blk = jnp.arange(n_sel)[None, :]
    cur = (pos // SEL_BLOCK)[:, None]
    forced = (blk == 0) | (blk == cur) | (blk == cur - 1)
    valid = blk * SEL_BLOCK <= pos[:, None]
    imp = jnp.where(forced, imp + FORCE_BONUS, imp)
    imp = jnp.where(valid, imp, NEG)
    top_n = min(SEL_TOP_N, n_sel)
    _, sel_idx = lax.top_k(imp, top_n)

    C = SEL_QCHUNK
    n_chunk = S // C
    K = top_n * SEL_BLOCK
    q_ch = q_rot.reshape(B, G, R, n_chunk, C, dh).transpose(3, 0, 1, 2, 4, 5)
    idx_ch = sel_idx.reshape(B, G, n_chunk, C, top_n).transpose(2, 0, 1, 3, 4)
    t_ch = pos.reshape(n_chunk, C)
    b_ix = jnp.arange(B)[:, None, None]
    g_ix = jnp.arange(G)[None, :, None]
    tok_off = jnp.arange(SEL_BLOCK)

    def sel_chunk(args):
        qc, ic, tc = args
        tok = (ic[..., None] * SEL_BLOCK + tok_off).reshape(B, G, C * K)
        ks = k_slc[b_ix, g_ix, tok].reshape(B, G, C, K, dh)
        vs = v_slc[b_ix, g_ix, tok].reshape(B, G, C, K, dh)
        mask = (tok.reshape(B, G, C, K) <= tc[:, None])[:, :, None]
        s = jnp.einsum("bgrcd,bgckd->bgrck", qc, ks) * scale
        p = masked_softmax(s, mask)
        return jnp.einsum("bgrck,bgckd->bgrcd", p.astype(vs.dtype), vs)

    o_slc = lax.map(sel_chunk, (q_ch, idx_ch, t_ch))
    o_slc = o_slc.transpose(1, 2, 3, 0, 4, 5).reshape(B, G, R, S, dh)

    QB = WIN_QBLOCK
    nqb = S // QB
    span = WINDOW + QB
    k_pad = jnp.pad(k_win, ((0, 0), (0, 0), (WINDOW, 0), (0, 0)))
    v_pad = jnp.pad(v_win, ((0, 0), (0, 0), (WINDOW, 0), (0, 0)))
    q_wb = q_rot.reshape(B, G, R, nqb, QB, dh).transpose(3, 0, 1, 2, 4, 5)

    def win_block(args):
        qb, i = args
        start = i * QB
        kb = lax.dynamic_slice_in_dim(k_pad, start, span, axis=2)
        vb = lax.dynamic_slice_in_dim(v_pad, start, span, axis=2)
        tq = start + jnp.arange(QB)
        tk = start - WINDOW + jnp.arange(span)
        diff = tq[:, None] - tk[None, :]
        mask = (diff >= 0) & (diff < WINDOW) & (tk[None, :] >= 0)
        s = jnp.einsum("bgrqd,bgkd->bgrqk", qb, kb) * scale
        p = masked_softmax(s, mask)
        return jnp.einsum("bgrqk,bgkd->bgrqd", p.astype(vb.dtype), vb)

    o_win = lax.map(win_block, (q_wb, jnp.arange(nqb)))
    o_win = o_win.transpose(1, 2, 3, 0, 4, 5).reshape(B, G, R, S, dh)

    o = (gates[..., 0:1] * o_cmp + gates[..., 1:2] * o_slc + gates[..., 2:3] * o_win).astype(h.dtype)
    o = o.transpose(0, 3, 1, 2, 4).reshape(B, S, N_HEADS * dh)
    return o @ w_o


def setup_inputs(seed: int = 0) -> dict:
    key = jax.random.key(seed)
    ks = jax.random.split(key, 32)
    nA, nB = N_CONV_LAYERS, N_ATTN_LAYERS
    D = D_MODEL

    def nrm(k, shape, scale):
        return jax.random.normal(k, shape, jnp.float32) * scale

    def gain(k, shape):
        return 1.0 + 0.05 * jax.random.normal(k, shape, jnp.float32)

    kv_width = N_KV_TENSORS * N_KV_HEADS * HEAD_DIM
    in_width = N_HEADS * HEAD_DIM + N_BRANCH * N_HEADS
    flat = CMP_BLOCK * HEAD_DIM
    return {
        "x": nrm(ks[0], (BATCH, SEQ, D), 1.0),
        "ln_mix_pre": gain(ks[1], (DEPTH, D)),
        "ln_mix_post": gain(ks[2], (DEPTH, D)),
        "ln_mlp_pre": gain(ks[3], (DEPTH, D)),
        "ln_mlp_post": gain(ks[4], (DEPTH, D)),
        "conv_w_pw1": nrm(ks[5], (nA, D, 2 * D), D ** -0.5),
        "conv_b_pw1": nrm(ks[6], (nA, 2 * D), 0.01),
        "conv_w_dw": nrm(ks[7], (nA, CONV_WIDTH, D), CONV_WIDTH ** -0.5),
        "conv_b_dw": nrm(ks[8], (nA, D), 0.01),
        "conv_ln_g": gain(ks[9], (nA, D)),
        "conv_ln_b": nrm(ks[10], (nA, D), 0.01),
        "conv_w_pw2": nrm(ks[11], (nA, D, D), D ** -0.5),
        "conv_b_pw2": nrm(ks[12], (nA, D), 0.01),
        "ln_kv": gain(ks[13], (D,)),
        "w_kv": nrm(ks[14], (D, kv_width), D ** -0.5),
        "cmp_pe_k": nrm(ks[15], (CMP_BLOCK, HEAD_DIM), 0.02),
        "cmp_pe_v": nrm(ks[16], (CMP_BLOCK, HEAD_DIM), 0.02),
        "cmp_k_w1": nrm(ks[17], (flat, CMP_HIDDEN), flat ** -0.5),
        "cmp_k_w2": nrm(ks[18], (CMP_HIDDEN, HEAD_DIM), CMP_HIDDEN ** -0.5),
        "cmp_v_w1": nrm(ks[19], (flat, CMP_HIDDEN), flat ** -0.5),
        "cmp_v_w2": nrm(ks[20], (CMP_HIDDEN, HEAD_DIM), CMP_HIDDEN ** -0.5),
        "attn_w_in": nrm(ks[21], (nB, D, in_width), D ** -0.5),
        "attn_w_o": nrm(ks[22], (nB, N_HEADS * HEAD_DIM, D), (N_HEADS * HEAD_DIM) ** -0.5),
        "mlp_w_up": nrm(ks[23], (DEPTH, D, D_FF), D ** -0.5),
        "mlp_w_down": nrm(ks[24], (DEPTH, D_FF, D), D_FF ** -0.5),
    }


def reference(x, ln_mix_pre, ln_mix_post, ln_mlp_pre, ln_mlp_post,
              conv_w_pw1, conv_b_pw1, conv_w_dw, conv_b_dw, conv_ln_g, conv_ln_b, conv_w_pw2, conv_b_pw2,
              ln_kv, w_kv, cmp_pe_k, cmp_pe_v, cmp_k_w1, cmp_k_w2, cmp_v_w1, cmp_v_w2,
              attn_w_in, attn_w_o, mlp_w_up, mlp_w_down):
    h = x
    kv = None
    for layer in range(DEPTH):
        hn = rmsnorm(h, ln_mix_pre[layer])
        if layer < N_CONV_LAYERS:
            a = conv_module(hn, conv_w_pw1[layer], conv_b_pw1[layer], conv_w_dw[layer], conv_b_dw[layer],
                            conv_ln_g[layer], conv_ln_b[layer], conv_w_pw2[layer], conv_b_pw2[layer])
        else:
            j = layer - N_CONV_LAYERS
            a = nsa_mixer(hn, kv, attn_w_in[j], attn_w_o[j])
        h = h + rmsnorm(a, ln_mix_post[layer])
        m = sq_relu_mlp(rmsnorm(h, ln_mlp_pre[layer]), mlp_w_up[layer], mlp_w_down[layer])
        h = h + rmsnorm(m, ln_mlp_post[layer])
        if layer == N_CONV_LAYERS - 1:
            kv = shared_kv(rmsnorm(h, ln_kv), w_kv, cmp_pe_k, cmp_pe_v,
                           cmp_k_w1, cmp_k_w2, cmp_v_w1, cmp_v_w2)
    return h
```

```python
import functools

import jax
import jax.numpy as jnp
from jax import lax
from jax.experimental import pallas as pl
from jax.experimental.pallas import tpu as pltpu

F32 = jnp.float32
BF16 = jnp.bfloat16

HEAD_DIM = 128
N_KV_HEADS = 4
N_KV_TENSORS = 6
N_BRANCH = 3
ROPE_DIM = HEAD_DIM // 4
ROPE_THETA = 500000.0
CMP_BLOCK = 32
CMP_STRIDE = 16
SEL_BLOCK = 64
SEL_SHIFT = SEL_BLOCK.bit_length() - 1
SEL_TOP_N = 16
WINDOW = 512
FORCE_BONUS = 1000.0
EPS = 1e-6
NEG = -1e30

LANES = 128
VMEM_LIMIT = 48 * 1024 * 1024

NT_DIMS = (((1,), (1,)), ((), ()))


def _params(*sem):
    return pltpu.CompilerParams(dimension_semantics=sem, vmem_limit_bytes=VMEM_LIMIT)


def _sigmoid(x):
    return 1.0 / (1.0 + jnp.exp(-x))


def _rms_scale(x):
    return lax.rsqrt(jnp.mean(x * x, axis=-1, keepdims=True) + EPS)


def _tile(n, want):
    t = min(n, want)
    assert n % t == 0, (n, want)
    return t


def _prenorm_kernel(x_ref, g_ref, o_ref):
    x = x_ref[...]
    o_ref[...] = (x * _rms_scale(x) * g_ref[...]).astype(o_ref.dtype)


def prenorm(x, g):
    m, d = x.shape
    tm = _tile(m, 512)
    return pl.pallas_call(
        _prenorm_kernel,
        grid=(m // tm,),
        in_specs=[pl.BlockSpec((tm, d), lambda i: (i, 0)), pl.BlockSpec((1, d), lambda i: (0, 0))],
        out_specs=pl.BlockSpec((tm, d), lambda i: (i, 0)),
        out_shape=jax.ShapeDtypeStruct((m, d), BF16),
        compiler_params=_params("parallel"),
        name="prenorm",
    )(x, g.reshape(1, d))


def _mm_kernel(*refs, act, has_bias):
    if has_bias:
        a_ref, w_ref, b_ref, o_ref = refs
    else:
        a_ref, w_ref, o_ref = refs
    y = jnp.dot(a_ref[...], w_ref[...], preferred_element_type=F32)
    if has_bias:
        y = y + b_ref[...]
    if act == "relu2":
        y = jnp.square(jnp.maximum(y, 0.0))
    elif act == "sigmoid":
        y = _sigmoid(y)
    o_ref[...] = y.astype(o_ref.dtype)


def mm(a, w, b=None, *, act="none", out_dtype=F32, tm=1024, tn=1024, name="mm"):
    m, k = a.shape
    n = w.shape[1]
    tm, tn = _tile(m, tm), _tile(n, tn)
    in_specs = [pl.BlockSpec((tm, k), lambda i, j: (i, 0)), pl.BlockSpec((k, tn), lambda i, j: (0, j))]
    args = [a, w]
    if b is not None:
        in_specs.append(pl.BlockSpec((1, tn), lambda i, j: (0, j)))
        args.append(b.reshape(1, n).astype(F32))
    return pl.pallas_call(
        functools.partial(_mm_kernel, act=act, has_bias=b is not None),
        grid=(m // tm, n // tn),
        in_specs=in_specs,
        out_specs=pl.BlockSpec((tm, tn), lambda i, j: (i, j)),
        out_shape=jax.ShapeDtypeStruct((m, n), out_dtype),
        compiler_params=_params("parallel", "parallel"),
        name=name,
    )(*args)


def _glu_kernel(a_ref, wa_ref, wg_ref, ba_ref, bg_ref, o_ref):
    a = a_ref[...]
    ya = jnp.dot(a, wa_ref[...], preferred_element_type=F32) + ba_ref[...]
    yg = jnp.dot(a, wg_ref[...], preferred_element_type=F32) + bg_ref[...]
    o_ref[...] = (ya * _sigmoid(yg)).astype(o_ref.dtype)


def mm_glu(a, w, b, *, tm=1024, tn=512):
    m, k = a.shape
    n = w.shape[1] // 2
    tm, tn = _tile(m, tm), _tile(n, tn)
    nj = n // tn
    b2 = b.reshape(1, 2 * n).astype(F32)
    return pl.pallas_call(
        _glu_kernel,
        grid=(m // tm, nj),
        in_specs=[
            pl.BlockSpec((tm, k), lambda i, j: (i, 0)),
            pl.BlockSpec((k, tn), lambda i, j: (0, j)),
            pl.BlockSpec((k, tn), lambda i, j: (0, j + nj)),
            pl.BlockSpec((1, tn), lambda i, j: (0, j)),
            pl.BlockSpec((1, tn), lambda i, j: (0, j + nj)),
        ],
        out_specs=pl.BlockSpec((tm, tn), lambda i, j: (i, j)),
        out_shape=jax.ShapeDtypeStruct((m, n), F32),
        compiler_params=_params("parallel", "parallel"),
        name="mm_glu",
    )(a, w, w, b2, b2)


def _mm_resnorm_kernel(*refs, nk, n_next, has_bias):
    a_ref, w_ref = refs[0], refs[1]
    pos = 2
    b_ref = None
    if has_bias:
        b_ref = refs[pos]
        pos += 1
    r_ref, gp_ref = refs[pos], refs[pos + 1]
    pos += 2
    gn_ref = None
    if n_next:
        gn_ref = refs[pos]
        pos += 1
    h_ref = refs[pos]
    hn_refs = refs[pos + 1:pos + 1 + n_next]
    acc_ref = refs[pos + 1 + n_next] if nk > 1 else None

    def finish(y):
        if has_bias:
            y = y + b_ref[...]
        h = r_ref[...] + y * _rms_scale(y) * gp_ref[...]
        h_ref[...] = h
        if n_next:
            hs = h * _rms_scale(h)
            for j in range(n_next):
                hn_refs[j][...] = (hs * gn_ref[j:j + 1, :]).astype(BF16)

    part = jnp.dot(a_ref[...], w_ref[...], preferred_element_type=F32)
    if nk == 1:
        finish(part)
    else:
        k = pl.program_id(1)

        @pl.when(k == 0)
        def _():
            acc_ref[...] = part

        @pl.when(k > 0)
        def _():
            acc_ref[...] += part

        @pl.when(k == nk - 1)
        def _():
            finish(acc_ref[...])


def mm_resnorm(a, w, b, resid, g_post, gains_next, *, tm=512, tk=1024, name="mm_resnorm"):
    m, kdim = a.shape
    n = w.shape[1]
    tm, tk = _tile(m, tm), _tile(kdim, tk)
    nk = kdim // tk
    n_next = 0 if gains_next is None else gains_next.shape[0]
    in_specs = [pl.BlockSpec((tm, tk), lambda i, k: (i, k)), pl.BlockSpec((tk, n), lambda i, k: (k, 0))]
    args = [a, w]
    if b is not None:
        in_specs.append(pl.BlockSpec((1, n), lambda i, k: (0, 0)))
        args.append(b.reshape(1, n).astype(F32))
    in_specs += [pl.BlockSpec((tm, n), lambda i, k: (i, 0)), pl.BlockSpec((1, n), lambda i, k: (0, 0))]
    args += [resid, g_post.reshape(1, n)]
    if n_next:
        in_specs.append(pl.BlockSpec((n_next, n), lambda i, k: (0, 0)))
        args.append(gains_next)
    out_specs = [pl.BlockSpec((tm, n), lambda i, k: (i, 0))] * (1 + n_next)
    out_shape = [jax.ShapeDtypeStruct((m, n), F32)] + [jax.ShapeDtypeStruct((m, n), BF16)] * n_next
    scratch = [pltpu.VMEM((tm, n), F32)] if nk > 1 else []
    outs = pl.pallas_call(
        functools.partial(_mm_resnorm_kernel, nk=nk, n_next=n_next, has_bias=b is not None),
        grid=(m // tm, nk),
        in_specs=in_specs,
        out_specs=out_specs,
        out_shape=out_shape,
        scratch_shapes=scratch,
        compiler_params=_params("parallel", "arbitrary"),
        name=name,
    )(*args)
    return outs[0], list(outs[1:])


CONV_HALO = 32
CONV_ROWS = 32
CONV_COLS = 256


def _dwconv_kernel(u_ref, halo_ref, w_ref, b_ref, g_ref, beta_ref, o_ref, buf_ref, y_ref, *, taps, ts, d):
    i = pl.program_id(1)
    halo = halo_ref[0]
    buf_ref[0:CONV_HALO, :] = jnp.where(i == 0, 0.0, halo)
    buf_ref[CONV_HALO:, :] = u_ref[0]
    base = CONV_HALO - (taps - 1)
    for c0 in range(0, d, CONV_COLS):
        cs = slice(c0, c0 + CONV_COLS)
        for r0 in range(0, ts, CONV_ROWS):
            acc = jnp.zeros((CONV_ROWS, CONV_COLS), F32)
            for j in range(taps):
                acc = acc + w_ref[j:j + 1, cs] * buf_ref[base + r0 + j:base + r0 + j + CONV_ROWS, cs]
            y_ref[r0:r0 + CONV_ROWS, cs] = acc + b_ref[:, cs]
    y = y_ref[...]
    mu = jnp.mean(y, axis=-1, keepdims=True)
    yc = y - mu
    var = jnp.mean(yc * yc, axis=-1, keepdims=True)
    z = yc * lax.rsqrt(var + EPS) * g_ref[...] + beta_ref[...]
    o_ref[0] = (z * _sigmoid(z)).astype(o_ref.dtype)


def dwconv_ln_silu(u, w_dw, b_dw, ln_g, ln_b, *, ts=128):
    bsz, s, d = u.shape
    taps = w_dw.shape[0]
    assert taps - 1 <= CONV_HALO and d % CONV_COLS == 0
    ts = _tile(s, ts)
    assert ts % CONV_HALO == 0 and ts % CONV_ROWS == 0
    hb = ts // CONV_HALO
    vec = lambda v: v.reshape(1, d).astype(F32)
    return pl.pallas_call(
        functools.partial(_dwconv_kernel, taps=taps, ts=ts, d=d),
        grid=(bsz, s // ts),
        in_specs=[
            pl.BlockSpec((1, ts, d), lambda b, i: (b, i, 0)),
            pl.BlockSpec((1, CONV_HALO, d), lambda b, i: (b, jnp.maximum(i * hb - 1, 0), 0)),
            pl.BlockSpec((taps, d), lambda b, i: (0, 0)),
            pl.BlockSpec((1, d), lambda b, i: (0, 0)),
            pl.BlockSpec((1, d), lambda b, i: (0, 0)),
            pl.BlockSpec((1, d), lambda b, i: (0, 0)),
        ],
        out_specs=pl.BlockSpec((1, ts, d), lambda b, i: (b, i, 0)),
        out_shape=jax.ShapeDtypeStruct((bsz, s, d), BF16),
        scratch_shapes=[pltpu.VMEM((ts + CONV_HALO, d), F32), pltpu.VMEM((ts, d), F32)],
        compiler_params=_params("parallel", "parallel"),
        name="dwconv_ln_silu",
    )(u, u, w_dw.astype(F32), vec(b_dw), vec(ln_g), vec(ln_b))


def _compress_kernel(z_ref, pe_ref, w1_ref, w2_ref, o_ref, *, rows):
    z = z_ref[0]
    half = z.shape[1]
    top = jnp.dot((z + pe_ref[0:1, :]).astype(BF16), w1_ref[0:half, :], preferred_element_type=F32)
    bot = jnp.dot((z + pe_ref[1:2, :]).astype(BF16), w1_ref[half:, :], preferred_element_type=F32)
    bot_next = jnp.concatenate([bot[1:], bot[:1]], axis=0)
    hid = top + bot_next
    hid = hid * _sigmoid(hid)
    o_ref[0] = jnp.dot(hid.astype(BF16), w2_ref[...], preferred_element_type=F32).astype(o_ref.dtype)


def compress(z, pe, w1, w2):
    n, rows, half = z.shape
    hid = w1.shape[1]
    return pl.pallas_call(
        functools.partial(_compress_kernel, rows=rows),
        grid=(n,),
        in_specs=[
            pl.BlockSpec((1, rows, half), lambda i: (i, 0, 0)),
            pl.BlockSpec((2, half), lambda i: (0, 0)),
            pl.BlockSpec((2 * half, hid), lambda i: (0, 0)),
            pl.BlockSpec((hid, HEAD_DIM), lambda i: (0, 0)),
        ],
        out_specs=pl.BlockSpec((1, rows, HEAD_DIM), lambda i: (i, 0, 0)),
        out_shape=jax.ShapeDtypeStruct((n, rows, HEAD_DIM), BF16),
        compiler_params=_params("parallel"),
        name="compress",
    )(z, pe.reshape(2, half).astype(F32), w1.astype(BF16), w2.astype(BF16))


def rope_tables(s):
    half = ROPE_DIM // 2
    inv = ROPE_THETA ** (-jnp.arange(0, ROPE_DIM, 2, dtype=F32) / ROPE_DIM)
    ang = jnp.arange(s, dtype=F32)[:, None] * inv[None, :]
    cos, sin = jnp.cos(ang), jnp.sin(ang)
    zeros = jnp.zeros((s, HEAD_DIM - ROPE_DIM), F32)
    zh = jnp.zeros((s, half), F32)
    c = jnp.concatenate([cos, cos, jnp.ones_like(zeros)], axis=-1)
    s1 = jnp.concatenate([zh, sin, zeros], axis=-1)
    s2 = jnp.concatenate([-sin, zh, zeros], axis=-1)
    return c, s1, s2


def _rope(x, c, s1, s2):
    half = ROPE_DIM // 2
    return x * c + pltpu.roll(x, half, 1) * s1 + pltpu.roll(x, HEAD_DIM - half, 1) * s2


def _rope_kernel(x_ref, c_ref, s1_ref, s2_ref, o_ref, *, heads):
    c, s1, s2 = c_ref[...], s1_ref[...], s2_ref[...]
    for h in range(heads):
        hs = slice(h * HEAD_DIM, (h + 1) * HEAD_DIM)
        o_ref[:, hs] = _rope(x_ref[:, hs], c, s1, s2).astype(o_ref.dtype)


def rope_cols(x, col_block, width, tables, s, *, tm=512):
    m = x.shape[0]
    tm = _tile(s, tm)
    ns = s // tm
    tab = pl.BlockSpec((tm, HEAD_DIM), lambda i: (i % ns, 0))
    return pl.pallas_call(
        functools.partial(_rope_kernel, heads=width // HEAD_DIM),
        grid=(m // tm,),
        in_specs=[pl.BlockSpec((tm, width), lambda i: (i, col_block)), tab, tab, tab],
        out_specs=pl.BlockSpec((tm, width), lambda i: (i, 0)),
        out_shape=jax.ShapeDtypeStruct((m, width), BF16),
        compiler_params=_params("parallel"),
        name="rope_k",
    )(x, *tables)


def _attn_kernel(q_ref, c_ref, s1_ref, s2_ref, kc_ref, vct_ref, ks_ref, vst_ref, kw_ref, vwt_ref, gate_ref,
                 o_ref, m_ref, l_ref, acc_ref, *, tq, tk, rep, n_sel, top_n):
    i = pl.program_id(2)
    t0 = i * tq
    cols = rep * tq
    scale = HEAD_DIM ** -0.5
    n_cmp_pad = kc_ref.shape[2]

    q = q_ref[0]
    c, s1, s2 = c_ref[...], s1_ref[...], s2_ref[...]
    heads = [q[:, r * HEAD_DIM:(r + 1) * HEAD_DIM] for r in range(rep)]
    qb = jnp.concatenate(heads, axis=0).astype(BF16)
    qr = jnp.concatenate([_rope(h, c, s1, s2) for h in heads], axis=0).astype(BF16)

    t_q = t0 + lax.broadcasted_iota(jnp.int32, (1, tq), 1)
    t_all = jnp.concatenate([t_q] * rep, axis=1)

    s_c = lax.dot_general(kc_ref[0, 0], qb, NT_DIMS, preferred_element_type=F32) * scale
    n_idx = lax.broadcasted_iota(jnp.int32, (n_cmp_pad, 1), 0)
    mask_c = (n_idx * CMP_STRIDE + (CMP_BLOCK - 1)) <= t_all
    s_c = jnp.where(mask_c, s_c, NEG)
    m_c = jnp.max(s_c, axis=0, keepdims=True)
    p_c = jnp.where(mask_c, jnp.exp(s_c - m_c), 0.0)
    l_c = jnp.sum(p_c, axis=0, keepdims=True)
    p_c = p_c * (1.0 / jnp.where(l_c > 0.0, l_c, 1.0))
    o_c = jnp.dot(vct_ref[0, 0], p_c.astype(BF16), preferred_element_type=F32)

    p_sum = p_c[:, 0:tq]
    for r in range(1, rep):
        p_sum = p_sum + p_c[:, r * tq:(r + 1) * tq]
    j_idx = lax.broadcasted_iota(jnp.int32, (n_sel, 1), 0)
    n_lane = lax.broadcasted_iota(jnp.int32, (1, n_cmp_pad), 1)
    overlap = jnp.where((n_lane * CMP_STRIDE < j_idx * SEL_BLOCK + SEL_BLOCK)
                        & (n_lane * CMP_STRIDE + CMP_BLOCK > j_idx * SEL_BLOCK), 1.0, 0.0).astype(BF16)
    p_hi = p_sum.astype(BF16)
    rem = p_sum - p_hi.astype(F32)
    p_mid = rem.astype(BF16)
    p_lo = (rem - p_mid.astype(F32)).astype(BF16)
    imp = (jnp.dot(overlap, p_hi, preferred_element_type=F32)
           + jnp.dot(overlap, p_mid, preferred_element_type=F32)
           + jnp.dot(overlap, p_lo, preferred_element_type=F32))
    j_full = lax.broadcasted_iota(jnp.int32, (n_sel, tq), 0)
    cur = jnp.broadcast_to(t_q >> SEL_SHIFT, (n_sel, tq))
    forced = (j_full == 0) | (j_full == cur) | (j_full == cur - 1)
    imp = jnp.where(forced, imp + FORCE_BONUS, imp)
    imp = jnp.where(j_full * SEL_BLOCK <= t_q, imp, NEG)
    rank = jnp.zeros((n_sel, tq), F32)
    for b in range(n_sel):
        row = imp[b:b + 1, :]
        tie = jnp.where(j_full > b, 1.0, 0.0)
        rank = rank + jnp.where(row > imp, 1.0, jnp.where(row == imp, tie, 0.0))
    sel = jnp.where(rank < top_n, 1.0, 0.0).astype(BF16)

    blk_lane = lax.broadcasted_iota(jnp.int32, (1, n_sel), 1)

    def flash_step(kt, k_ref, vt_ref, mask_fn):
        k0 = kt * tk
        kk = k_ref[0, pl.ds(pl.multiple_of(k0, tk), tk), :]
        s = lax.dot_general(kk, qr, NT_DIMS, preferred_element_type=F32) * scale
        key = k0 + lax.broadcasted_iota(jnp.int32, (tk, 1), 0)
        mk = mask_fn(key)
        mk = jnp.concatenate([mk] * rep, axis=1) > 0.5
        s = jnp.where(mk, s, NEG)
        m_old = m_ref[...]
        m_new = jnp.maximum(m_old, jnp.max(s, axis=0, keepdims=True))
        alpha = jnp.exp(m_old - m_new)
        p = jnp.where(mk, jnp.exp(s - m_new), 0.0)
        l_ref[...] = alpha * l_ref[...] + jnp.sum(p, axis=0, keepdims=True)
        acc_ref[...] = alpha * acc_ref[...] + jnp.dot(vt_ref[0, kt], p.astype(BF16),
                                                      preferred_element_type=F32)
        m_ref[...] = m_new

    def run_branch(k_ref, vt_ref, lo, hi, mask_fn):
        m_ref[...] = jnp.full((1, cols), NEG, F32)
        l_ref[...] = jnp.zeros((1, cols), F32)
        acc_ref[...] = jnp.zeros((HEAD_DIM, cols), F32)

        def body(kt, carry):
            flash_step(kt, k_ref, vt_ref, mask_fn)
            return carry

        lax.fori_loop(lo, hi, body, 0)
        return acc_ref[...] * (1.0 / l_ref[...])

    def sel_mask(key):
        onehot = jnp.where((key >> SEL_SHIFT) == blk_lane, 1.0, 0.0).astype(BF16)
        chosen = jnp.dot(onehot, sel, preferred_element_type=F32)
        return jnp.where(key <= t_q, chosen, 0.0)

    def win_mask(key):
        return jnp.where((key <= t_q) & (key > t_q - WINDOW), 1.0, 0.0)

    last = (t0 + tq - 1) // tk + 1
    o_s = run_branch(ks_ref, vst_ref, 0, last, sel_mask)
    o_w = run_branch(kw_ref, vwt_ref, jnp.maximum(t0 - (WINDOW - 1), 0) // tk, last, win_mask)

    g = gate_ref[0, 0, 0]
    o_t = g[0:1, :] * o_c + g[1:2, :] * o_s + g[2:3, :] * o_w
    o = o_t.T
    for r in range(rep):
        o_ref[0, :, r * HEAD_DIM:(r + 1) * HEAD_DIM] = o[r * tq:(r + 1) * tq, :].astype(o_ref.dtype)


def nsa_attention(q, tables, k_cmp, v_cmp_t, k_slc, v_slc_t, k_win, v_win_t, gates_t, *, tq, tk):
    bsz, s, hd = q.shape
    g = N_KV_HEADS
    rep = hd // HEAD_DIM // g
    n_sel = s // SEL_BLOCK
    top_n = min(SEL_TOP_N, n_sel)
    ncp = k_cmp.shape[2]
    cols = rep * tq
    tab = pl.BlockSpec((tq, HEAD_DIM), lambda b, h, i: (i, 0))
    kspec = pl.BlockSpec((1, s, HEAD_DIM), lambda b, h, i: (b, 0, h))
    vspec = pl.BlockSpec((1, s // tk, HEAD_DIM, tk), lambda b, h, i: (b, 0, h, 0))
    return pl.pallas_call(
        functools.partial(_attn_kernel, tq=tq, tk=tk, rep=rep, n_sel=n_sel, top_n=top_n),
        grid=(bsz, g, s // tq),
        in_specs=[
            pl.BlockSpec((1, tq, rep * HEAD_DIM), lambda b, h, i: (b, i, h)),
            tab, tab, tab,
            pl.BlockSpec((1, 1, ncp, HEAD_DIM), lambda b, h, i: (b, h, 0, 0)),
            pl.BlockSpec((1, 1, HEAD_DIM, ncp), lambda b, h, i: (b, h, 0, 0)),
            kspec, vspec, kspec, vspec,
            pl.BlockSpec((1, 1, 1, N_BRANCH, cols), lambda b, h, i: (b, h, i, 0, 0)),
        ],
        out_specs=pl.BlockSpec((1, tq, rep * HEAD_DIM), lambda b, h, i: (b, i, h)),
        out_shape=jax.ShapeDtypeStruct((bsz, s, hd), BF16),
        scratch_shapes=[pltpu.VMEM((1, cols), F32), pltpu.VMEM((1, cols), F32),
                        pltpu.VMEM((HEAD_DIM, cols), F32)],
        compiler_params=_params("parallel", "parallel", "arbitrary"),
        name="nsa_attention",
    )(q, *tables, k_cmp, v_cmp_t, k_slc, v_slc_t, k_win, v_win_t, gates_t)


def kernel(x, ln_mix_pre, ln_mix_post, ln_mlp_pre, ln_mlp_post, conv_w_pw1, conv_b_pw1, conv_w_dw, conv_b_dw,
           conv_ln_g, conv_ln_b, conv_w_pw2, conv_b_pw2, ln_kv, w_kv, cmp_pe_k, cmp_pe_v, cmp_k_w1, cmp_k_w2,
           cmp_v_w1, cmp_v_w2, attn_w_in, attn_w_o, mlp_w_up, mlp_w_down):
    bsz, s, d = x.shape
    m = bsz * s
    depth = ln_mix_pre.shape[0]
    n_conv = conv_w_pw1.shape[0]
    g = N_KV_HEADS
    n_heads = d // HEAD_DIM
    rep = n_heads // g
    kvw = g * HEAD_DIM
    tq = tk = _tile(s, 256)
    assert s % SEL_BLOCK == 0 and s % CMP_STRIDE == 0 and WINDOW % tk == 0

    bf = lambda w: w.astype(BF16)
    tables = rope_tables(s)

    h = x.reshape(m, d)
    hn = prenorm(h, ln_mix_pre[0])
    kv_ops = None
    for layer in range(depth):
        if layer < n_conv:
            u = mm_glu(hn, bf(conv_w_pw1[layer]), conv_b_pw1[layer])
            u = dwconv_ln_silu(u.reshape(bsz, s, d), conv_w_dw[layer], conv_b_dw[layer],
                               conv_ln_g[layer], conv_ln_b[layer]).reshape(m, d)
            h, (hn,) = mm_resnorm(u, bf(conv_w_pw2[layer]), conv_b_pw2[layer], h, ln_mix_post[layer],
                                  ln_mlp_pre[layer][None], tk=d, name="pw2_resnorm")
        else:
            j = layer - n_conv
            w_in = attn_w_in[j]
            q = mm(hn, bf(w_in[:, :n_heads * HEAD_DIM]), name="q_proj")
            n_gate = N_BRANCH * n_heads
            w_gate = jnp.pad(w_in[:, n_heads * HEAD_DIM:], ((0, 0), (0, LANES - n_gate)))
            gates = mm(hn, bf(w_gate), act="sigmoid", tn=LANES, name="gate_proj")[:, :n_gate]
            gates_t = gates.reshape(bsz, s // tq, tq, g, rep, N_BRANCH).transpose(0, 3, 1, 5, 4, 2)
            gates_t = gates_t.reshape(bsz, g, s // tq, N_BRANCH, rep * tq)
            o = nsa_attention(q.reshape(bsz, s, n_heads * HEAD_DIM), tables, *kv_ops, gates_t, tq=tq, tk=tk)
            h, (hn,) = mm_resnorm(o.reshape(m, n_heads * HEAD_DIM), bf(attn_w_o[j]), None, h, ln_mix_post[layer],
                                  ln_mlp_pre[layer][None], tk=d, name="wo_resnorm")

        hid = mm(hn, bf(mlp_w_up[layer]), act="relu2", out_dtype=BF16, name="mlp_up")
        if layer == depth - 1:
            gains = None
        elif layer == n_conv - 1:
            gains = jnp.stack([ln_mix_pre[layer + 1], ln_kv])
        else:
            gains = ln_mix_pre[layer + 1][None]
        h, nxt = mm_resnorm(hid, bf(mlp_w_down[layer]), None, h, ln_mlp_post[layer], gains, name="mlp_down_resnorm")
        if nxt:
            hn = nxt[0]

        if layer == n_conv - 1:
            kv = mm(nxt[1], bf(w_kv), name="kv_proj")
            part = lambda t: kv[:, t * kvw:(t + 1) * kvw]

            def cmp_rows(t):
                zz = part(t).reshape(bsz, s // CMP_STRIDE, CMP_STRIDE, g, HEAD_DIM).transpose(0, 3, 1, 2, 4)
                return zz.reshape(bsz * g, s // CMP_STRIDE, CMP_STRIDE * HEAD_DIM)

            ncp = s // CMP_STRIDE
            k_cmp = compress(cmp_rows(0), cmp_pe_k, cmp_k_w1, cmp_k_w2).reshape(bsz, g, ncp, HEAD_DIM)
            v_cmp = compress(cmp_rows(1), cmp_pe_v, cmp_v_w1, cmp_v_w2).reshape(bsz, g, ncp, HEAD_DIM)
            v_cmp_t = v_cmp.transpose(0, 1, 3, 2)

            def v_tiles(t):
                return bf(part(t)).reshape(bsz, s // tk, tk, kvw).transpose(0, 1, 3, 2)

            k_slc = rope_cols(kv, 2, kvw, tables, s).reshape(bsz, s, kvw)
            k_win = rope_cols(kv, 4, kvw, tables, s).reshape(bsz, s, kvw)
            kv_ops = (k_cmp, v_cmp_t, k_slc, v_tiles(3), k_win, v_tiles(5))
    return h.reshape(bsz, s, d)
```
